```python
import math
import jax, jax.numpy as jnp
from jax import lax
import numpy as np

D_MODEL = 1024
BATCH = 4
SEQ = 8192
DEPTH = 1

HEAD_DIM = 64
N_HEADS_A = 8
N_KV_A = 2
N_HEADS_B = 8
WIDTH_A = N_HEADS_A * HEAD_DIM
WIDTH_B = N_HEADS_B * HEAD_DIM
MIX_WIDTH = WIDTH_A + WIDTH_B
KV_WIDTH_A = N_KV_A * HEAD_DIM
IN_WIDTH = WIDTH_A + 2 * KV_WIDTH_A + 3 * WIDTH_B
GRID_W = 64
WIN_R_MAX = 8
WIN_C = 16
Q_BLOCK = 128
ROPE_THETA = 10000.0
D_FF = -(-8 * D_MODEL // (3 * 256)) * 256
EPS = 1e-6

kernel_name = "hymba_gqa_axialrope_natten2d_swiglu"


def rmsnorm(x, g):
    xf = x.astype(jnp.float32)
    y = xf * lax.rsqrt(jnp.mean(xf * xf, axis=-1, keepdims=True) + EPS)
    return (y * g.astype(jnp.float32)).astype(x.dtype)


def axial_rope_tables(seq):
    t = jnp.arange(seq)
    row = (t // GRID_W).astype(jnp.float32)
    col = (t % GRID_W).astype(jnp.float32)
    half = HEAD_DIM // 2
    inv_freq = ROPE_THETA ** (-jnp.arange(0, half, 2, dtype=jnp.float32) / half)
    ang_r = row[:, None] * inv_freq[None, :]
    ang_c = col[:, None] * inv_freq[None, :]
    return jnp.cos(ang_r), jnp.sin(ang_r), jnp.cos(ang_c), jnp.sin(ang_c)


def rope_half(x, cos, sin):
    x1, x2 = jnp.split(x, 2, axis=-1)
    c = cos[None, :, None, :]
    s = sin[None, :, None, :]
    return jnp.concatenate([x1 * c - x2 * s, x2 * c + x1 * s], axis=-1)


def apply_axial_rope(x, tabs):
    cr, sr, cc, sc = tabs
    xf = x.astype(jnp.float32)
    half = HEAD_DIM // 2
    out = jnp.concatenate([rope_half(xf[..., :half], cr, sr), rope_half(xf[..., half:], cc, sc)], axis=-1)
    return out.astype(x.dtype)


def gqa_axial_attention(q, k, v, q_gain, k_gain):
    b, s, _ = q.shape
    g = N_HEADS_A // N_KV_A
    q = rmsnorm(q.reshape(b, s, N_HEADS_A, HEAD_DIM), q_gain)
    k = rmsnorm(k.reshape(b, s, N_KV_A, HEAD_DIM), k_gain)
    v = v.reshape(b, s, N_KV_A, HEAD_DIM)
    tabs = axial_rope_tables(s)
    q = apply_axial_rope(q, tabs) * (1.0 / math.sqrt(HEAD_DIM))
    k = apply_axial_rope(k, tabs)
    nb = s // Q_BLOCK
    qb = q.reshape(b, nb, Q_BLOCK, N_KV_A, g, HEAD_DIM).transpose(1, 0, 2, 3, 4, 5)

    def block(qblk):
        sc = jnp.einsum('bqkgd,bskd->bkgqs', qblk, k)
        p = jax.nn.softmax(sc.astype(jnp.float32), axis=-1).astype(v.dtype)
        return jnp.einsum('bkgqs,bskd->bqkgd', p, v)

    o = lax.map(block, qb)
    return o.transpose(1, 0, 2, 3, 4, 5).reshape(b, s, WIDTH_A)


def neighbourhood_attention_2d(q, k, v, rpb):
    b, s, _ = q.shape
    rows = s // GRID_W
    wr = min(WIN_R_MAX, rows)
    nk = wr * WIN_C
    q = q.reshape(b, s, N_HEADS_B, HEAD_DIM) * (1.0 / math.sqrt(HEAD_DIM))
    k = k.reshape(b, s, N_HEADS_B, HEAD_DIM)
    v = v.reshape(b, s, N_HEADS_B, HEAD_DIM)
    t = jnp.arange(s)
    r = t // GRID_W
    c = t % GRID_W
    rs = jnp.clip(r - wr // 2, 0, rows - wr)
    cs = jnp.clip(c - WIN_C // 2, 0, GRID_W - WIN_C)
    kr = rs[:, None, None] + jnp.arange(wr)[None, :, None]
    kc = cs[:, None, None] + jnp.arange(WIN_C)[None, None, :]
    idx = (kr * GRID_W + kc).reshape(s, nk)
    dr = kr - r[:, None, None] + (WIN_R_MAX - 1)
    dc = kc - c[:, None, None] + (WIN_C - 1)
    bidx = (dr * (2 * WIN_C - 1) + dc).reshape(s, nk)
    rpb_flat = rpb.reshape(N_HEADS_B, -1)
    nb = s // Q_BLOCK
    qb = q.reshape(b, nb, Q_BLOCK, N_HEADS_B, HEAD_DIM).transpose(1, 0, 2, 3, 4)
    idx_b = idx.reshape(nb, Q_BLOCK, nk)
    bidx_b = bidx.reshape(nb, Q_BLOCK, nk)

    def block(args):
        qblk, ib, bb = args
        kg = jnp.take(k, ib, axis=1)
        vg = jnp.take(v, ib, axis=1)
        sc = jnp.einsum('bqhd,bqnhd->bhqn', qblk, kg).astype(jnp.float32)
        sc = sc + rpb_flat[:, bb].astype(jnp.float32)[None]
        p = jax.nn.softmax(sc, axis=-1).astype(v.dtype)
        return jnp.einsum('bhqn,bqnhd->bqhd', p, vg)

    o = lax.map(block, (qb, idx_b, bidx_b))
    return o.transpose(1, 0, 2, 3, 4).reshape(b, s, WIDTH_B)


def setup_inputs(seed: int = 0) -> dict:
    key = jax.random.key(seed)
    ks = jax.random.split(key, 16)
    f32 = jnp.float32

    def w(k, shape, fan_in):
        return jax.random.normal(k, shape, f32) * fan_in ** -0.5

    def gain(k, shape):
        return 1.0 + 0.01 * jax.random.normal(k, shape, f32)

    return {
        "x": jax.random.normal(ks[0], (BATCH, SEQ, D_MODEL), f32),
        "norm_mix": gain(ks[1], (DEPTH, D_MODEL)),
        "w_in": w(ks[2], (DEPTH, D_MODEL, IN_WIDTH), D_MODEL),
        "q_norm_a": gain(ks[3], (DEPTH, HEAD_DIM)),
        "k_norm_a": gain(ks[4], (DEPTH, HEAD_DIM)),
        "rpb_b": 0.1 * jax.random.normal(ks[5], (DEPTH, N_HEADS_B, 2 * WIN_R_MAX - 1, 2 * WIN_C - 1), f32),
        "out_norm_a": gain(ks[6], (DEPTH, WIDTH_A)),
        "out_norm_b": gain(ks[7], (DEPTH, WIDTH_B)),
        "w_out": w(ks[8], (DEPTH, MIX_WIDTH, D_MODEL), MIX_WIDTH),
        "norm_ffn": gain(ks[9], (DEPTH, D_MODEL)),
        "w_gate": w(ks[10], (DEPTH, D_MODEL, D_FF), D_MODEL),
        "w_up": w(ks[11], (DEPTH, D_MODEL, D_FF), D_MODEL),
        "w_down": w(ks[12], (DEPTH, D_FF, D_MODEL), D_FF),
        "norm_final": gain(ks[13], (D_MODEL,)),
    }


def reference(x, norm_mix, w_in, q_norm_a, k_norm_a, rpb_b, out_norm_a, out_norm_b,
              w_out, norm_ffn, w_gate, w_up, w_down, norm_final):
    splits = [WIDTH_A, WIDTH_A + KV_WIDTH_A, WIDTH_A + 2 * KV_WIDTH_A,
              WIDTH_A + 2 * KV_WIDTH_A + WIDTH_B, WIDTH_A + 2 * KV_WIDTH_A + 2 * WIDTH_B]
    for l in range(DEPTH):
        h = rmsnorm(x, norm_mix[l])
        proj = jnp.einsum('bsd,de->bse', h, w_in[l])
        q_a, k_a, v_a, q_b, k_b, v_b = jnp.split(proj, splits, axis=-1)
        o_a = gqa_axial_attention(q_a, k_a, v_a, q_norm_a[l], k_norm_a[l])
        o_b = neighbourhood_attention_2d(q_b, k_b, v_b, rpb_b[l])
        mixed = jnp.concatenate([rmsnorm(o_a, out_norm_a[l]), rmsnorm(o_b, out_norm_b[l])], axis=-1)
        x = x + jnp.einsum('bse,ed->bsd', mixed, w_out[l])
        h = rmsnorm(x, norm_ffn[l])
        gate = jnp.einsum('bsd,df->bsf', h, w_gate[l])
        up = jnp.einsum('bsd,df->bsf', h, w_up[l])
        x = x + jnp.einsum('bsf,fd->bsd', jax.nn.silu(gate) * up, w_down[l])
    return rmsnorm(x, norm_final)
```

```python
import functools
import math

import jax
import jax.numpy as jnp
import numpy as np
from jax import lax
from jax.experimental import pallas as pl
from jax.experimental.pallas import tpu as pltpu

HEAD_DIM = 64
N_HEADS_A = 8
N_KV_A = 2
GROUP_A = N_HEADS_A // N_KV_A
N_HEADS_B = 8
WIDTH_A = N_HEADS_A * HEAD_DIM
WIDTH_B = N_HEADS_B * HEAD_DIM
KV_WIDTH_A = N_KV_A * HEAD_DIM
GRID_W = 64
WIN_R_MAX = 8
WIN_C = 16
Q_BLOCK = 128
ROPE_THETA = 10000.0
EPS = 1e-6
LOG2E = math.log2(math.e)
ROPE_FREQS = HEAD_DIM // 4

NEG_BIG = -1e30

PROJ_TM = 512
GQA_TQ = 256
GQA_TK = 256
FFN_TM = 256
NB_Q_ROWS = Q_BLOCK // GRID_W
VMEM_LIMIT_BYTES = 56 * 1024 * 1024


def _rmsnorm_rows(x, g):
  ms = jnp.mean(x * x, axis=-1, keepdims=True)
  return x * lax.rsqrt(ms + EPS) * g


def _headnorm_rope_t(xt, gain, cr, sr, cc, sc):
  ss = jnp.sum(xt * xt, axis=1, keepdims=True)
  xn = xt * lax.rsqrt(ss * (1.0 / HEAD_DIM) + EPS) * gain[None]
  f = ROPE_FREQS
  x1r, x2r, x1c, x2c = xn[:, 0:f], xn[:, f:2 * f], xn[:, 2 * f:3 * f], xn[:, 3 * f:4 * f]
  return jnp.concatenate(
      [x1r * cr - x2r * sr, x2r * cr + x1r * sr, x1c * cc - x2c * sc, x2c * cc + x1c * sc],
      axis=1)


def _proj_kernel(x_ref, g_ref, w_ref, qg_ref, kg_ref, cr_ref, sr_ref, cc_ref, sc_ref,
                 qat_ref, ka_ref, vat_ref, qb_ref, kb_ref, vb_ref):
  tm = x_ref.shape[0]
  h = _rmsnorm_rows(x_ref[...], g_ref[...]).astype(jnp.bfloat16)
  proj = jnp.dot(h, w_ref[...], preferred_element_type=jnp.float32)
  o0 = WIDTH_A
  o1 = o0 + KV_WIDTH_A
  o2 = o1 + KV_WIDTH_A
  o3 = o2 + WIDTH_B
  o4 = o3 + WIDTH_B
  cr, sr, cc, sc = cr_ref[...], sr_ref[...], cc_ref[...], sc_ref[...]

  qt = proj[:, 0:o0].T.reshape(N_HEADS_A, HEAD_DIM, tm)
  qt = _headnorm_rope_t(qt, qg_ref[...], cr, sr, cc, sc) * (LOG2E / math.sqrt(HEAD_DIM))
  qat_ref[...] = qt.reshape(WIDTH_A, tm).astype(qat_ref.dtype)

  kt = proj[:, o0:o1].T.reshape(N_KV_A, HEAD_DIM, tm)
  kt = _headnorm_rope_t(kt, kg_ref[...], cr, sr, cc, sc)
  kn = kt.reshape(KV_WIDTH_A, tm).T
  for j in range(N_KV_A):
    ka_ref[j] = kn[:, j * HEAD_DIM:(j + 1) * HEAD_DIM].astype(ka_ref.dtype)

  vat_ref[...] = proj[:, o1:o2].T.reshape(N_KV_A, HEAD_DIM, tm).astype(vat_ref.dtype)

  qb_ref[...] = (proj[:, o2:o3] * (LOG2E / math.sqrt(HEAD_DIM))).astype(qb_ref.dtype)
  kb_ref[...] = proj[:, o3:o4].astype(kb_ref.dtype)
  vb_ref[...] = proj[:, o4:].astype(vb_ref.dtype)


def _const_spec(shape):
  n = len(shape)
  return pl.BlockSpec(shape, lambda *_: (0,) * n, pipeline_mode=pl.Buffered(1))


def _projection(x, norm_g, w_in, q_gain, k_gain, tabs):
  b, s, d = x.shape
  tm = PROJ_TM
  in_width = w_in.shape[1]
  bf = jnp.bfloat16
  tab_spec = pl.BlockSpec((ROPE_FREQS, tm), lambda bi, i: (0, i))
  out_shapes = (
      jax.ShapeDtypeStruct((b, WIDTH_A, s), bf),
      jax.ShapeDtypeStruct((b, N_KV_A, s, HEAD_DIM), bf),
      jax.ShapeDtypeStruct((b, N_KV_A, HEAD_DIM, s), bf),
      jax.ShapeDtypeStruct((b, s, WIDTH_B), bf),
      jax.ShapeDtypeStruct((b, s, WIDTH_B), bf),
      jax.ShapeDtypeStruct((b, s, WIDTH_B), bf),
  )
  row_spec = pl.BlockSpec((None, tm, WIDTH_B), lambda bi, i: (bi, i, 0))
  return pl.pallas_call(
      _proj_kernel,
      grid=(b, s // tm),
      in_specs=[
          pl.BlockSpec((None, tm, d), lambda bi, i: (bi, i, 0)),
          _const_spec((1, d)),
          _const_spec((d, in_width)),
          _const_spec((HEAD_DIM, 1)),
          _const_spec((HEAD_DIM, 1)),
          tab_spec, tab_spec, tab_spec, tab_spec,
      ],
      out_specs=(
          pl.BlockSpec((None, WIDTH_A, tm), lambda bi, i: (bi, 0, i)),
          pl.BlockSpec((None, N_KV_A, tm, HEAD_DIM), lambda bi, i: (bi, 0, i, 0)),
          pl.BlockSpec((None, N_KV_A, HEAD_DIM, tm), lambda bi, i: (bi, 0, 0, i)),
          row_spec, row_spec, row_spec,
      ),
      out_shape=out_shapes,
      compiler_params=pltpu.CompilerParams(
          dimension_semantics=("arbitrary", "arbitrary"),
          vmem_limit_bytes=VMEM_LIMIT_BYTES),
      name="proj",
  )(x, norm_g.reshape(1, d), w_in.astype(bf), q_gain.reshape(HEAD_DIM, 1),
    k_gain.reshape(HEAD_DIM, 1), *tabs)


def _gqa_kernel(qt_ref, k_ref, vt_ref, ot_ref, m_sc, l_sc, acc_sc):
  tq = qt_ref.shape[1]
  s = k_ref.shape[0]
  tk = GQA_TK
  m_sc[...] = jnp.full(m_sc.shape, -jnp.inf, jnp.float32)
  l_sc[...] = jnp.zeros(l_sc.shape, jnp.float32)
  acc_sc[...] = jnp.zeros(acc_sc.shape, jnp.float32)

  def body(j, carry):
    off = pl.multiple_of(j * tk, tk)
    kb = k_ref[pl.ds(off, tk), :]
    vb = vt_ref[:, pl.ds(off, tk)]
    for g in range(GROUP_A):
      qg = qt_ref[g * HEAD_DIM:(g + 1) * HEAD_DIM, :]
      st = jnp.dot(kb, qg, preferred_element_type=jnp.float32)
      m_old = m_sc[g]
      m_new = jnp.maximum(m_old, jnp.max(st, axis=0, keepdims=True))
      p = jnp.exp2(st - m_new)
      alpha = jnp.exp2(m_old - m_new)
      l_sc[g] = alpha * l_sc[g] + jnp.sum(p, axis=0, keepdims=True)
      m_sc[g] = m_new
      pv = jnp.dot(vb, p.astype(jnp.bfloat16), preferred_element_type=jnp.float32)
      acc_sc[g] = alpha * acc_sc[g] + pv
    return carry

  lax.fori_loop(0, s // tk, body, 0)
  for g in range(GROUP_A):
    ot_ref[g * HEAD_DIM:(g + 1) * HEAD_DIM, :] = (acc_sc[g] / l_sc[g]).astype(ot_ref.dtype)


def _gqa_attention(qat, ka, vat):
  b, _, s = qat.shape
  n = b * N_KV_A
  gw = GROUP_A * HEAD_DIM
  tq = GQA_TQ
  out = pl.pallas_call(
      _gqa_kernel,
      grid=(n, s // tq),
      in_specs=[
          pl.BlockSpec((None, gw, tq), lambda i, j: (i, 0, j)),
          pl.BlockSpec((None, s, HEAD_DIM), lambda i, j: (i, 0, 0)),
          pl.BlockSpec((None, HEAD_DIM, s), lambda i, j: (i, 0, 0)),
      ],
      out_specs=pl.BlockSpec((None, gw, tq), lambda i, j: (i, 0, j)),
      out_shape=jax.ShapeDtypeStruct((n, gw, s), jnp.float32),
      scratch_shapes=[
          pltpu.VMEM((GROUP_A, 1, tq), jnp.float32),
          pltpu.VMEM((GROUP_A, 1, tq), jnp.float32),
          pltpu.VMEM((GROUP_A, HEAD_DIM, tq), jnp.float32),
      ],
      compiler_params=pltpu.CompilerParams(
          dimension_semantics=("arbitrary", "arbitrary"),
          vmem_limit_bytes=VMEM_LIMIT_BYTES),
      name="gqa",
  )(qat.reshape(n, gw, s), ka.reshape(n, s, HEAD_DIM), vat.reshape(n, HEAD_DIM, s))
  return out.reshape(b, WIDTH_A, s)


def _nb_geometry(seq):
  rows = seq // GRID_W
  wr = min(WIN_R_MAX, rows)
  slab_rows = min(rows, wr + NB_Q_ROWS - 1)
  return rows, wr, slab_rows


def _nb_slab_start_row(blk, rows, wr, slab_rows):
  return jnp.clip(blk * NB_Q_ROWS - wr // 2, 0, rows - slab_rows)


def _nb_block_types(seq):
  rows, wr, slab_rows = _nb_geometry(seq)
  nblk = seq // Q_BLOCK
  keys = []
  for blk in range(nblk):
    base = int(np.clip(blk * NB_Q_ROWS - wr // 2, 0, rows - slab_rows))
    geo = tuple(int(np.clip(blk * NB_Q_ROWS + i - wr // 2, 0, rows - wr)) - base
                for i in range(NB_Q_ROWS)) + (blk * NB_Q_ROWS - base,)
    keys.append(geo)
  reps, type_of = [], []
  for blk, key in enumerate(keys):
    first = keys.index(key)
    if first == blk:
      reps.append(blk)
    type_of.append(reps.index(first))
  return reps, type_of


def _nb_bias_tables(rpb, seq):
  rows, wr, slab_rows = _nb_geometry(seq)
  reps, type_of = _nb_block_types(seq)
  blk = jnp.asarray(reps, jnp.int32)[:, None, None]
  qi = jnp.arange(Q_BLOCK, dtype=jnp.int32)[None, :, None]
  kn = jnp.arange(slab_rows * GRID_W, dtype=jnp.int32)[None, None, :]
  r = blk * NB_Q_ROWS + qi // GRID_W
  c = qi % GRID_W
  base = jnp.clip(blk * NB_Q_ROWS - wr // 2, 0, rows - slab_rows)
  kr = base + kn // GRID_W
  kc = kn % GRID_W
  rs = jnp.clip(r - wr // 2, 0, rows - wr)
  cs = jnp.clip(c - WIN_C // 2, 0, GRID_W - WIN_C)
  valid = (kr >= rs) & (kr < rs + wr) & (kc >= cs) & (kc < cs + WIN_C)
  dr = jnp.clip(kr - r + (WIN_R_MAX - 1), 0, 2 * WIN_R_MAX - 2)
  dc = jnp.clip(kc - c + (WIN_C - 1), 0, 2 * WIN_C - 2)
  bias = rpb[:, dr, dc].astype(jnp.float32) * LOG2E
  bias = jnp.where(valid[None], bias, NEG_BIG)
  return jnp.transpose(bias, (1, 0, 2, 3)), jnp.asarray(type_of, jnp.int32)


def _nb_kernel(type_ref, q_ref, k_ref, v_ref, bias_ref, o_ref, *, rows, wr, slab_rows):
  del type_ref
  blk = pl.program_id(1)
  start = pl.multiple_of(_nb_slab_start_row(blk, rows, wr, slab_rows) * GRID_W, GRID_W)
  slab = slab_rows * GRID_W
  q = q_ref[...]
  ks = k_ref[pl.ds(start, slab), :]
  vs = v_ref[pl.ds(start, slab), :]
  outs = []
  for h in range(N_HEADS_B):
    sl = slice(h * HEAD_DIM, (h + 1) * HEAD_DIM)
    sc = lax.dot_general(q[:, sl], ks[:, sl], (((1,), (1,)), ((), ())),
                         preferred_element_type=jnp.float32)
    sc = sc + bias_ref[h]
    m = jnp.max(sc, axis=-1, keepdims=True)
    p = jnp.exp2(sc - m)
    l = jnp.sum(p, axis=-1, keepdims=True)
    o = jnp.dot(p.astype(jnp.bfloat16), vs[:, sl], preferred_element_type=jnp.float32)
    outs.append(o / l)
  o_ref[...] = jnp.concatenate(outs, axis=1).astype(o_ref.dtype)


def _nb_attention(qb, kb, vb, rpb):
  b, s, w = qb.shape
  rows, wr, slab_rows = _nb_geometry(s)
  slab = slab_rows * GRID_W
  bias, type_of = _nb_bias_tables(rpb, s)
  grid_spec = pltpu.PrefetchScalarGridSpec(
      num_scalar_prefetch=1,
      grid=(b, s // Q_BLOCK),
      in_specs=[
          pl.BlockSpec((None, Q_BLOCK, w), lambda bi, i, t: (bi, i, 0)),
          pl.BlockSpec((None, s, w), lambda bi, i, t: (bi, 0, 0)),
          pl.BlockSpec((None, s, w), lambda bi, i, t: (bi, 0, 0)),
          pl.BlockSpec((None, N_HEADS_B, Q_BLOCK, slab), lambda bi, i, t: (t[i], 0, 0, 0)),
      ],
      out_specs=pl.BlockSpec((None, Q_BLOCK, w), lambda bi, i, t: (bi, i, 0)),
  )
  return pl.pallas_call(
      functools.partial(_nb_kernel, rows=rows, wr=wr, slab_rows=slab_rows),
      grid_spec=grid_spec,
      out_shape=jax.ShapeDtypeStruct((b, s, w), jnp.float32),
      compiler_params=pltpu.CompilerParams(
          dimension_semantics=("arbitrary", "arbitrary"),
          vmem_limit_bytes=VMEM_LIMIT_BYTES),
      name="natten",
  )(type_of, qb, kb, vb, bias)


def _ffn_kernel(x_ref, oat_ref, ob_ref, ga_ref, gb_ref, wo_ref, gf_ref, wg_ref, wu_ref, wd_ref,
                gl_ref, y_ref):
  bf = jnp.bfloat16
  na = _rmsnorm_rows(oat_ref[...].T, ga_ref[...]).astype(bf)
  nb = _rmsnorm_rows(ob_ref[...], gb_ref[...]).astype(bf)
  x1 = (x_ref[...]
        + jnp.dot(na, wo_ref[0:WIDTH_A, :], preferred_element_type=jnp.float32)
        + jnp.dot(nb, wo_ref[WIDTH_A:, :], preferred_element_type=jnp.float32))
  h = _rmsnorm_rows(x1, gf_ref[...]).astype(bf)
  gate = jnp.dot(h, wg_ref[...], preferred_element_type=jnp.float32)
  up = jnp.dot(h, wu_ref[...], preferred_element_type=jnp.float32)
  act = (gate * jax.nn.sigmoid(gate) * up).astype(bf)
  x2 = x1 + jnp.dot(act, wd_ref[...], preferred_element_type=jnp.float32)
  y_ref[...] = _rmsnorm_rows(x2, gl_ref[...]).astype(y_ref.dtype)


def _out_ffn(x, oat, ob, ga, gb, w_out, gf, w_gate, w_up, w_down, gl):
  b, s, d = x.shape
  tm = FFN_TM
  bf = jnp.bfloat16
  dff = w_gate.shape[1]
  return pl.pallas_call(
      _ffn_kernel,
      grid=(b, s // tm),
      in_specs=[
          pl.BlockSpec((None, tm, d), lambda bi, i: (bi, i, 0)),
          pl.BlockSpec((None, WIDTH_A, tm), lambda bi, i: (bi, 0, i)),
          pl.BlockSpec((None, tm, WIDTH_B), lambda bi, i: (bi, i, 0)),
          _const_spec((1, WIDTH_A)),
          _const_spec((1, WIDTH_B)),
          _const_spec((WIDTH_A + WIDTH_B, d)),
          _const_spec((1, d)),
          _const_spec((d, dff)),
          _const_spec((d, dff)),
          _const_spec((dff, d)),
          _const_spec((1, d)),
      ],
      out_specs=pl.BlockSpec((None, tm, d), lambda bi, i: (bi, i, 0)),
      out_shape=jax.ShapeDtypeStruct((b, s, d), x.dtype),
      compiler_params=pltpu.CompilerParams(
          dimension_semantics=("arbitrary", "arbitrary"),
          vmem_limit_bytes=VMEM_LIMIT_BYTES),
      name="out_ffn",
  )(x, oat, ob, ga.reshape(1, -1), gb.reshape(1, -1), w_out.astype(bf), gf.reshape(1, -1),
    w_gate.astype(bf), w_up.astype(bf), w_down.astype(bf), gl.reshape(1, -1))


def _rope_tables_t(seq):
  t = jnp.arange(seq)
  row = (t // GRID_W).astype(jnp.float32)
  col = (t % GRID_W).astype(jnp.float32)
  half = HEAD_DIM // 2
  inv_freq = ROPE_THETA ** (-jnp.arange(0, half, 2, dtype=jnp.float32) / half)
  ang_r = inv_freq[:, None] * row[None, :]
  ang_c = inv_freq[:, None] * col[None, :]
  return jnp.cos(ang_r), jnp.sin(ang_r), jnp.cos(ang_c), jnp.sin(ang_c)


def kernel(x, norm_mix, w_in, q_norm_a, k_norm_a, rpb_b, out_norm_a, out_norm_b, w_out, norm_ffn,
           w_gate, w_up, w_down, norm_final):
  assert w_in.shape[0] == 1, "the closing norm is fused into the (single) layer's last kernel"
  tabs = _rope_tables_t(x.shape[1])
  qat, ka, vat, qb, kb, vb = _projection(x, norm_mix[0], w_in[0], q_norm_a[0], k_norm_a[0], tabs)
  oat = _gqa_attention(qat, ka, vat)
  ob = _nb_attention(qb, kb, vb, rpb_b[0])
  return _out_ffn(x, oat, ob, out_norm_a[0], out_norm_b[0], w_out[0], norm_ffn[0], w_gate[0],
                  w_up[0], w_down[0], norm_final)
```
